```python
import math
import jax, jax.numpy as jnp
from jax import lax
import numpy as np

D_MODEL = 2048
BATCH = 1
SEQ = 8192
DEPTH = 2

F32 = jnp.float32
GRID_W = 64
CTX_LEN = 256
GROUP_W = 512
MIX_W = 4 * GROUP_W
CONV_W = GROUP_W
CONV_K = 3
DIFF_HEADS = 4
DIFF_QK_DIM = 64
DIFF_V_DIM = 2 * DIFF_QK_DIM
DIFF_QK_W = DIFF_HEADS * 2 * DIFF_QK_DIM
DIFF_V_W = DIFF_HEADS * DIFF_V_DIM
DIFF_SCALE = DIFF_QK_DIM ** -0.5
SSD_HEADS = 8
SSD_HEAD_DIM = 64
SSD_INNER = SSD_HEADS * SSD_HEAD_DIM
SSD_GROUPS = 2
SSD_STATE = 128
SSD_BC_W = SSD_GROUPS * SSD_STATE
SSD_XBC_W = SSD_INNER + 2 * SSD_BC_W
SSD_DT_W = 2 * SSD_HEADS
SSD_CONV_K = 3
SSD_CHUNK = 128
MLA_HEADS = 4
MLA_NOPE = 128
MLA_ROPE = 64
MLA_V = 128
MLA_Q_RANK = 384
MLA_KV_RANK = 256
MLA_SCALE = (MLA_NOPE + MLA_ROPE) ** -0.5
ROPE_DIM = 64
ROPE_BASE = 10000.0
Q_BLOCK = 128
D_FF = -(-8 * D_MODEL // (3 * 256)) * 256
ALPHA = (2 * DEPTH) ** 0.25
BETA = (8 * DEPTH) ** -0.25
LN_EPS = 1e-5
RMS_EPS = 1e-6
IN_SIZES = (CONV_W, CONV_W, CONV_W,
            DIFF_QK_W, DIFF_QK_W, DIFF_V_W,
            SSD_INNER, SSD_XBC_W, SSD_DT_W,
            MLA_Q_RANK, MLA_KV_RANK, MLA_ROPE)
IN_W = sum(IN_SIZES)
IN_SPLITS = tuple(int(v) for v in np.cumsum(IN_SIZES)[:-1])

kernel_name = 'hymba_style_hybrid_dit_block'


def _layer_norm(x, g, b):
    xf = x.astype(F32)
    xc = xf - jnp.mean(xf, axis=-1, keepdims=True)
    var = jnp.mean(xc * xc, axis=-1, keepdims=True)
    return (xc * lax.rsqrt(var + LN_EPS) * g + b).astype(x.dtype)


def _rms(x):
    xf = x.astype(F32)
    return (xf * lax.rsqrt(jnp.mean(xf * xf, axis=-1, keepdims=True) + RMS_EPS)).astype(x.dtype)


def _dwconv_centred(x, w):
    k, ch = w.shape
    return lax.conv_general_dilated(x, w[:, None, :].astype(x.dtype), window_strides=(1,),
                                    padding=[((k - 1) // 2, k // 2)],
                                    dimension_numbers=('NWC', 'WIO', 'NWC'),
                                    feature_group_count=ch)


def _rope_2d_tables(n_tokens, dim):
    n_rows = n_tokens // GRID_W
    row = jnp.repeat(jnp.arange(n_rows, dtype=F32), GRID_W)
    col = jnp.tile(jnp.arange(GRID_W, dtype=F32), n_rows)
    axis_dim = dim // 2
    inv_freq = jnp.power(ROPE_BASE, -jnp.arange(0, axis_dim, 2, dtype=F32) / axis_dim)
    ang_r = row[:, None] * inv_freq
    ang_c = col[:, None] * inv_freq
    ang = jnp.concatenate([ang_r, ang_r, ang_c, ang_c], axis=-1)
    return jnp.cos(ang), jnp.sin(ang)


def _apply_rope_2d(x, cos, sin):
    half = x.shape[-1] // 2
    qtr = half // 2
    def rot(v):
        return jnp.concatenate([-v[..., qtr:], v[..., :qtr]], axis=-1)
    rotated = jnp.concatenate([rot(x[..., :half]), rot(x[..., half:])], axis=-1)
    return (x * cos[:, None, :] + rotated * sin[:, None, :]).astype(x.dtype)


def _to_blocks(a):
    b, s = a.shape[:2]
    return jnp.moveaxis(a.reshape((b, s // Q_BLOCK, Q_BLOCK) + a.shape[2:]), 1, 0)


def _from_blocks(a):
    a = jnp.moveaxis(a, 0, 1)
    return a.reshape((a.shape[0], a.shape[1] * a.shape[2]) + a.shape[3:])


def _flip_seq(a, reverse):
    return jnp.flip(a, axis=1) if reverse else a


def _segsum(a):
    t = a.shape[-1]
    a_rep = jnp.broadcast_to(a[..., :, None], a.shape + (t,))
    seg = jnp.cumsum(jnp.where(jnp.tril(jnp.ones((t, t), bool), -1), a_rep, 0.0), axis=-2)
    return jnp.where(jnp.tril(jnp.ones((t, t), bool)), seg, -jnp.inf)


def _ssd_chunked(x, dt, a_neg, bm, cm, init_state, need_y):
    b, l, h, p = x.shape
    n = bm.shape[-1]
    nc = l // SSD_CHUNK
    xdt = (x.astype(F32) * dt[..., None]).reshape(b, nc, SSD_CHUNK, h, p)
    bc = bm.astype(F32).reshape(b, nc, SSD_CHUNK, h, n)
    a = jnp.moveaxis((dt * a_neg).reshape(b, nc, SSD_CHUNK, h), 3, 1)
    a_cum = jnp.cumsum(a, axis=-1)
    decay_to_end = jnp.exp(a_cum[..., -1:] - a_cum)
    states = jnp.einsum('bclhn,bhcl,bclhp->bchpn', bc, decay_to_end, xdt)
    states = jnp.concatenate([init_state.astype(F32)[:, None], states], axis=1)
    chunk_decay = jnp.exp(_segsum(jnp.pad(a_cum[..., -1], ((0, 0), (0, 0), (1, 0)))))
    states = jnp.einsum('bhzc,bchpn->bzhpn', chunk_decay, states)
    final_state = states[:, -1]
    if not need_y:
        return None, final_state
    cc = cm.astype(F32).reshape(b, nc, SSD_CHUNK, h, n)
    scores = jnp.einsum('bclhn,bcshn->bhcls', cc, bc) * jnp.exp(_segsum(a))
    y_diag = jnp.einsum('bhcls,bcshp->bclhp', scores, xdt)
    y_off = jnp.einsum('bclhn,bchpn,bhcl->bclhp', cc, states[:, :-1], jnp.exp(a_cum))
    return (y_diag + y_off).reshape(b, l, h, p), final_state


def _short_conv_mixer(lat, ctx, conv_w, need_ctx):
    def mix(bg, cg, u):
        return bg * _dwconv_centred(cg * u, conv_w)
    return mix(*lat), (mix(*ctx) if need_ctx else None)


def _diff_attention(lat, ctx, cos, sin, lam_params, subln_w, layer, need_ctx):
    q_l, k_l, v_l = lat
    q_c, k_c, v_c = ctx
    bsz, n_lat = q_l.shape[:2]
    n_ctx = q_c.shape[1]
    h, d = DIFF_HEADS, DIFF_QK_DIM
    lam_init = 0.8 - 0.6 * math.exp(-0.3 * layer)
    lp = lam_params.astype(F32)
    lam = jnp.exp(jnp.sum(lp[0] * lp[1])) - jnp.exp(jnp.sum(lp[2] * lp[3])) + lam_init

    def rope(a):
        return _apply_rope_2d(a.reshape(bsz, n_lat, 2 * h, d), cos, sin).reshape(bsz, n_lat, h, 2, d)

    k_ch = k_c.reshape(bsz, n_ctx, h, 2, d)
    v_ch = v_c.reshape(bsz, n_ctx, h, DIFF_V_DIM)
    keys = jnp.concatenate([rope(k_l), k_ch], axis=1)
    vals = jnp.concatenate([v_l.reshape(bsz, n_lat, h, DIFF_V_DIM), v_ch], axis=1)

    def attend(q, kk, vv):
        s = jnp.einsum('bqhmd,bkhmd->bhmqk', q, kk).astype(F32) * DIFF_SCALE
        p = jax.nn.softmax(s, axis=-1)
        attn = p[:, :, 0] - lam * p[:, :, 1]
        o = jnp.einsum('bhqk,bkhe->bqhe', attn.astype(vv.dtype), vv)
        o = _rms(o) * subln_w * (1.0 - lam_init)
        return o.reshape(o.shape[0], o.shape[1], h * DIFF_V_DIM)

    o_l = _from_blocks(lax.map(lambda qb: attend(qb, keys, vals), _to_blocks(rope(q_l))))
    o_c = attend(q_c.reshape(bsz, n_ctx, h, 2, d), k_ch, v_ch) if need_ctx else None
    return o_l, o_c


def _ssd_mixer(lat, ctx, conv_w, conv_b, dt_bias, a_log, d_skip, norm_w, need_ctx):
    a_neg = -jnp.exp(a_log.astype(F32))
    rep = SSD_HEADS // SSD_GROUPS

    def prep(z, xbc, dt_raw):
        b_, n_ = xbc.shape[:2]
        xbc = jax.nn.silu(_dwconv_centred(xbc, conv_w) + conv_b)
        xs, bm, cm = jnp.split(xbc, (SSD_INNER, SSD_INNER + SSD_BC_W), axis=-1)
        bm = jnp.repeat(bm.reshape(b_, n_, SSD_GROUPS, SSD_STATE), rep, axis=2)
        cm = jnp.repeat(cm.reshape(b_, n_, SSD_GROUPS, SSD_STATE), rep, axis=2)
        dt = jax.nn.softplus(dt_raw.reshape(b_, n_, 2, SSD_HEADS).astype(F32) + dt_bias.astype(F32))
        return z, xs.reshape(b_, n_, SSD_HEADS, SSD_HEAD_DIM), bm, cm, dt

    z_l, x_l, b_l, c_l, dt_l = prep(*lat)
    z_c, x_c, b_c, c_c, dt_c = prep(*ctx)
    bsz = x_l.shape[0]
    dsk = d_skip.astype(F32)[:, None]
    y_l = x_l.astype(F32) * dsk
    y_c = x_c.astype(F32) * dsk if need_ctx else None
    for direction in range(2):
        rev = direction == 1
        init = jnp.zeros((bsz, SSD_HEADS, SSD_HEAD_DIM, SSD_STATE), F32)
        yc_dir, ctx_state = _ssd_chunked(_flip_seq(x_c, rev), _flip_seq(dt_c[:, :, direction], rev), a_neg[direction],
                                         _flip_seq(b_c, rev), _flip_seq(c_c, rev), init, need_ctx)
        yl_dir, _ = _ssd_chunked(_flip_seq(x_l, rev), _flip_seq(dt_l[:, :, direction], rev), a_neg[direction],
                                 _flip_seq(b_l, rev), _flip_seq(c_l, rev), ctx_state, True)
        y_l = y_l + _flip_seq(yl_dir, rev)
        if need_ctx:
            y_c = y_c + _flip_seq(yc_dir, rev)

    def gate_norm(y, z):
        b_, n_ = z.shape[:2]
        g = y.reshape(b_, n_, SSD_INNER) * jax.nn.silu(z.astype(F32))
        g = _rms(g.reshape(b_, n_, SSD_GROUPS, SSD_INNER // SSD_GROUPS)).reshape(b_, n_, SSD_INNER)
        return (g * norm_w).astype(z.dtype)

    return gate_norm(y_l, z_l), (gate_norm(y_c, z_c) if need_ctx else None)


def _mla(lat, ctx, cos, sin, q_norm_w, kv_norm_w, w_q_up, w_kv_up, need_ctx):
    h = MLA_HEADS

    def queries(cq):
        b_, n_ = cq.shape[:2]
        q = ((_rms(cq) * q_norm_w) @ w_q_up).reshape(b_, n_, h, MLA_NOPE + MLA_ROPE)
        return q[..., :MLA_NOPE], q[..., MLA_NOPE:]

    def keys_values(ckv):
        b_, n_ = ckv.shape[:2]
        kv = ((_rms(ckv) * kv_norm_w) @ w_kv_up).reshape(b_, n_, h, MLA_NOPE + MLA_V)
        return kv[..., :MLA_NOPE], kv[..., MLA_NOPE:]

    cq_l, ckv_l, kr_l = lat
    cq_c, ckv_c, kr_c = ctx
    kn_l, v_l = keys_values(ckv_l)
    kn_c, v_c = keys_values(ckv_c)
    kr_l = _apply_rope_2d(kr_l[:, :, None, :], cos, sin)[:, :, 0]
    kn = jnp.concatenate([kn_l, kn_c], axis=1)
    kr = jnp.concatenate([kr_l, kr_c], axis=1)
    vv = jnp.concatenate([v_l, v_c], axis=1)

    def attend(qn, qr, k_n, k_r, v):
        s = (jnp.einsum('bqhd,bkhd->bhqk', qn, k_n) + jnp.einsum('bqhr,bkr->bhqk', qr, k_r)).astype(F32) * MLA_SCALE
        p = jax.nn.softmax(s, axis=-1)
        o = jnp.einsum('bhqk,bkhd->bqhd', p.astype(v.dtype), v)
        return o.reshape(o.shape[0], o.shape[1], h * MLA_V)

    qn_l, qr_l = queries(cq_l)
    qr_l = _apply_rope_2d(qr_l, cos, sin)
    o_l = _from_blocks(lax.map(lambda qb: attend(qb[0], qb[1], kn, kr, vv), (_to_blocks(qn_l), _to_blocks(qr_l))))
    if need_ctx:
        qn_c, qr_c = queries(cq_c)
        o_c = attend(qn_c, qr_c, kn_c, kr_c, v_c)
    else:
        o_c = None
    return o_l, o_c


def _token_mixers(h_lat, h_ctx, cos, sin, layer, need_ctx, w_in, conv_a_w, diff_lambda, diff_subln_w,
                  ssd_conv_w, ssd_conv_b, ssd_dt_bias, ssd_a_log, ssd_d, ssd_norm_w,
                  mla_q_norm_w, mla_kv_norm_w, w_q_up, w_kv_up, w_out):
    pl = jnp.split(h_lat @ w_in, IN_SPLITS, axis=-1)
    pc = jnp.split(h_ctx @ w_in, IN_SPLITS, axis=-1)
    a_l, a_c = _short_conv_mixer(pl[0:3], pc[0:3], conv_a_w, need_ctx)
    b_l, b_c = _diff_attention(pl[3:6], pc[3:6], cos, sin, diff_lambda, diff_subln_w, layer, need_ctx)
    c_l, c_c = _ssd_mixer(pl[6:9], pc[6:9], ssd_conv_w, ssd_conv_b, ssd_dt_bias, ssd_a_log, ssd_d, ssd_norm_w, need_ctx)
    d_l, d_c = _mla(pl[9:12], pc[9:12], cos, sin, mla_q_norm_w, mla_kv_norm_w, w_q_up, w_kv_up, need_ctx)
    y_l = jnp.concatenate([a_l, b_l, c_l, d_l], axis=-1) @ w_out
    y_c = (jnp.concatenate([a_c, b_c, c_c, d_c], axis=-1) @ w_out) if need_ctx else None
    return y_l, y_c


def _swiglu(h, w1, w2):
    g, u = jnp.split(h @ w1, 2, axis=-1)
    return (jax.nn.silu(g) * u) @ w2


def setup_inputs(seed: int = 0) -> dict:
    key = jax.random.key(seed)
    keys = iter(jax.random.split(key, 32))
    L = DEPTH

    def nrm(shape, scale):
        return jax.random.normal(next(keys), shape, F32) * scale

    def gain(shape):
        return 1.0 + nrm(shape, 0.02)

    dt0 = jnp.exp(jax.random.uniform(next(keys), (L, 2, SSD_HEADS), F32, math.log(1e-3), math.log(1e-1)))
    a_log = jnp.log(jax.random.uniform(next(keys), (L, 2, SSD_HEADS), F32, 1.0, 16.0))
    return {
        'x': nrm((BATCH, SEQ, D_MODEL), 1.0),
        'c': nrm((BATCH, D_MODEL), 1.0),
        'ctx': nrm((BATCH, CTX_LEN, D_MODEL), 1.0),
        'c_ctx': nrm((D_MODEL,), 1.0),
        'w_ada': nrm((L, D_MODEL, 6 * D_MODEL), 0.5 * D_MODEL ** -0.5),
        'b_ada': nrm((L, 6 * D_MODEL), 0.02),
        'w_in': nrm((L, D_MODEL, IN_W), D_MODEL ** -0.5),
        'conv_a_w': nrm((L, CONV_K, CONV_W), CONV_K ** -0.5),
        'diff_lambda': nrm((L, 4, DIFF_QK_DIM), 0.1),
        'diff_subln_w': gain((L, DIFF_V_DIM)),
        'ssd_conv_w': nrm((L, SSD_CONV_K, SSD_XBC_W), SSD_CONV_K ** -0.5),
        'ssd_conv_b': nrm((L, SSD_XBC_W), 0.02),
        'ssd_dt_bias': dt0 + jnp.log(-jnp.expm1(-dt0)),
        'ssd_a_log': a_log,
        'ssd_d': gain((L, SSD_HEADS)),
        'ssd_norm_w': gain((L, SSD_INNER)),
        'mla_q_norm_w': gain((L, MLA_Q_RANK)),
        'mla_kv_norm_w': gain((L, MLA_KV_RANK)),
        'w_q_up': nrm((L, MLA_Q_RANK, MLA_HEADS * (MLA_NOPE + MLA_ROPE)), MLA_Q_RANK ** -0.5),
        'w_kv_up': nrm((L, MLA_KV_RANK, MLA_HEADS * (MLA_NOPE + MLA_V)), MLA_KV_RANK ** -0.5),
        'w_out': nrm((L, MIX_W, D_MODEL), BETA * MIX_W ** -0.5),
        'ln1_g': gain((L, D_MODEL)),
        'ln1_b': nrm((L, D_MODEL), 0.02),
        'w_ffn_in': nrm((L, D_MODEL, 2 * D_FF), D_MODEL ** -0.5),
        'w_ffn_out': nrm((L, D_FF, D_MODEL), BETA * D_FF ** -0.5),
        'ln2_g': gain((L, D_MODEL)),
        'ln2_b': nrm((L, D_MODEL), 0.02),
    }


def reference(x, c, ctx, c_ctx, w_ada, b_ada, w_in, conv_a_w, diff_lambda, diff_subln_w,
              ssd_conv_w, ssd_conv_b, ssd_dt_bias, ssd_a_log, ssd_d, ssd_norm_w,
              mla_q_norm_w, mla_kv_norm_w, w_q_up, w_kv_up, w_out,
              ln1_g, ln1_b, w_ffn_in, w_ffn_out, ln2_g, ln2_b):
    cos, sin = _rope_2d_tables(x.shape[1], ROPE_DIM)
    for i in range(DEPTH):
        need_ctx = i < DEPTH - 1
        mod_l = (jax.nn.silu(c) @ w_ada[i] + b_ada[i])[:, None, :]
        mod_c = jax.nn.silu(c_ctx) @ w_ada[i] + b_ada[i]
        sh1, sc1, g1, sh2, sc2, g2 = jnp.split(mod_l, 6, axis=-1)
        csh1, csc1, cg1, csh2, csc2, cg2 = jnp.split(mod_c, 6, axis=-1)
        y_l, y_c = _token_mixers(x * (1.0 + sc1) + sh1, ctx * (1.0 + csc1) + csh1, cos, sin, i, need_ctx,
                                 w_in[i], conv_a_w[i], diff_lambda[i], diff_subln_w[i],
                                 ssd_conv_w[i], ssd_conv_b[i], ssd_dt_bias[i], ssd_a_log[i], ssd_d[i], ssd_norm_w[i],
                                 mla_q_norm_w[i], mla_kv_norm_w[i], w_q_up[i], w_kv_up[i], w_out[i])
        x = _layer_norm(ALPHA * x + g1 * y_l, ln1_g[i], ln1_b[i])
        x = _layer_norm(ALPHA * x + g2 * _swiglu(x * (1.0 + sc2) + sh2, w_ffn_in[i], w_ffn_out[i]), ln2_g[i], ln2_b[i])
        if need_ctx:
            ctx = _layer_norm(ALPHA * ctx + cg1 * y_c, ln1_g[i], ln1_b[i])
            ctx = _layer_norm(ALPHA * ctx + cg2 * _swiglu(ctx * (1.0 + csc2) + csh2, w_ffn_in[i], w_ffn_out[i]),
                              ln2_g[i], ln2_b[i])
    return x
```

```python
import functools
import math

import jax
import jax.numpy as jnp
from jax import lax
from jax.experimental import pallas as pl
from jax.experimental.pallas import tpu as pltpu

F32 = jnp.float32
BF16 = jnp.bfloat16
HIGHEST = lax.Precision.HIGHEST

GRID_W = 64
GROUP_W = 512
CONV_W = 512
DIFF_HEADS = 4
DIFF_QK_DIM = 64
DIFF_V_DIM = 128
DIFF_SCALE = DIFF_QK_DIM ** -0.5
SSD_HEADS = 8
SSD_HEAD_DIM = 64
SSD_INNER = 512
SSD_GROUPS = 2
SSD_STATE = 128
SSD_XBC_W = 1024
SSD_CHUNK = 128
MLA_HEADS = 4
MLA_NOPE = 128
MLA_ROPE = 64
MLA_V = 128
MLA_Q_RANK = 384
MLA_KV_RANK = 256
MLA_SCALE = (MLA_NOPE + MLA_ROPE) ** -0.5
ROPE_DIM = 64
ROPE_BASE = 10000.0
LN_EPS = 1e-5
RMS_EPS = 1e-6
LOG2E = math.log2(math.e)

LANE = 128
BF16_SUBLANE = 16
VMEM_LIMIT = 52 * 1024 * 1024

P_XBC = 0
P_BG = 1024
P_CG = 1536
P_U = 2048
P_Q = 2560
P_K = 3072
P_Z = 3584
P_CQ = 4096
P_CKV = 4608
P_KR = 4864
P_W = 5120


def _pick(n, cands):
    for c in cands:
        if n % c == 0:
            return c
    raise ValueError(f"no tile for {n} in {cands}")


def _cparams(sem):
    return pltpu.CompilerParams(dimension_semantics=sem, vmem_limit_bytes=VMEM_LIMIT)


def _silu(v):
    return v * jax.nn.sigmoid(v)


def _softplus(v):
    return jnp.maximum(v, 0.0) + jnp.log1p(jnp.exp(-jnp.abs(v)))


def _row_is_ctx(row0, tm, n_lat):
    rows = row0 + lax.broadcasted_iota(jnp.int32, (tm, 1), 0)
    return rows >= n_lat


def _mod_rows(mod, is_ctx, k, d):
    return jnp.where(is_ctx, mod[1:2, k * d:(k + 1) * d], mod[0:1, k * d:(k + 1) * d])


def _layer_norm(r, g, b):
    mu = jnp.mean(r, axis=-1, keepdims=True)
    rc = r - mu
    var = jnp.mean(rc * rc, axis=-1, keepdims=True)
    return rc * lax.rsqrt(var + LN_EPS) * g + b


def _ada_kernel(c_ref, w_ref, b_ref, o_ref):
    s = _silu(c_ref[...]).astype(BF16)
    o_ref[...] = jnp.dot(s, w_ref[...].astype(BF16), preferred_element_type=F32) + b_ref[...]


def _ada(cvec, w_ada, b_ada):
    n_layers, d, n = w_ada.shape
    tn = _pick(n, (1024, 512, 256, 128))
    return pl.pallas_call(
        _ada_kernel,
        grid=(n_layers, n // tn),
        in_specs=[
            pl.BlockSpec((8, d), lambda l, j: (0, 0)),
            pl.BlockSpec((None, d, tn), lambda l, j: (l, 0, j)),
            pl.BlockSpec((None, 1, tn), lambda l, j: (l, 0, j)),
        ],
        out_specs=pl.BlockSpec((None, 8, tn), lambda l, j: (l, 0, j)),
        out_shape=jax.ShapeDtypeStruct((n_layers, 8, n), F32),
        compiler_params=_cparams(("parallel", "parallel")),
        name="ada",
    )(cvec, w_ada, b_ada.reshape(n_layers, 1, n))


def _inproj_kernel(x_ref, mod_ref, w_ref, wvt_ref, wdtr_ref, wdtt_ref,
                   p_ref, vt_ref, dtr_ref, dtt_ref, h_scr, *, n_lat, tm, d):
    i = pl.program_id(0)
    j = pl.program_id(1)

    @pl.when(j == 0)
    def _():
        is_ctx = _row_is_ctx(i * tm, tm, n_lat)
        mod = mod_ref[...]
        sh = _mod_rows(mod, is_ctx, 0, d)
        sc = _mod_rows(mod, is_ctx, 1, d)
        h = (x_ref[...] * (1.0 + sc) + sh).astype(BF16)
        h_scr[...] = h
        nt = (((1,), (1,)), ((), ()))
        vt_ref[...] = lax.dot_general(wvt_ref[...], h, nt, preferred_element_type=F32).astype(BF16)
        dtr_ref[...] = jnp.dot(h, wdtr_ref[...], preferred_element_type=F32)
        dtt_ref[...] = lax.dot_general(wdtt_ref[...], h, nt, preferred_element_type=F32)

    p_ref[...] = jnp.dot(h_scr[...], w_ref[...], preferred_element_type=F32).astype(BF16)


def _inproj(xall, mods, layer, w_main, w_vt, w_dtr, w_dtt, n_lat):
    t, d = xall.shape
    tm = _pick(t, (768, 512, 256))
    tn = 512
    n6 = mods.shape[-1]
    kern = functools.partial(_inproj_kernel, n_lat=n_lat, tm=tm, d=d)
    return pl.pallas_call(
        kern,
        grid=(t // tm, P_W // tn),
        in_specs=[
            pl.BlockSpec((tm, d), lambda i, j: (i, 0)),
            pl.BlockSpec((None, 8, n6), lambda i, j: (layer, 0, 0)),
            pl.BlockSpec((d, tn), lambda i, j: (0, j)),
            pl.BlockSpec(w_vt.shape, lambda i, j: (0, 0)),
            pl.BlockSpec(w_dtr.shape, lambda i, j: (0, 0)),
            pl.BlockSpec(w_dtt.shape, lambda i, j: (0, 0)),
        ],
        out_specs=[
            pl.BlockSpec((tm, tn), lambda i, j: (i, j)),
            pl.BlockSpec((w_vt.shape[0], tm), lambda i, j: (0, i)),
            pl.BlockSpec((tm, w_dtr.shape[1]), lambda i, j: (i, 0)),
            pl.BlockSpec((w_dtt.shape[0], tm), lambda i, j: (0, i)),
        ],
        out_shape=[
            jax.ShapeDtypeStruct((t, P_W), BF16),
            jax.ShapeDtypeStruct((w_vt.shape[0], t), BF16),
            jax.ShapeDtypeStruct((t, w_dtr.shape[1]), F32),
            jax.ShapeDtypeStruct((w_dtt.shape[0], t), F32),
        ],
        scratch_shapes=[pltpu.VMEM((tm, d), BF16)],
        compiler_params=_cparams(("parallel", "arbitrary")),
        name="inproj",
    )(xall, mods, w_main, w_vt, w_dtr, w_dtt)


def _shifted(v, prev_row, next_row, row0, n_lat, n_tot):
    tm = v.shape[0]
    r = lax.broadcasted_iota(jnp.int32, (tm, 1), 0)
    g = r + row0
    up = jnp.where(r == 0, prev_row, pltpu.roll(v, 1, 0))
    up = jnp.where((g == 0) | (g == n_lat), 0.0, up)
    dn = jnp.where(r == tm - 1, next_row, pltpu.roll(v, tm - 1, 0))
    dn = jnp.where((g == n_lat - 1) | (g == n_tot - 1), 0.0, dn)
    return up, dn


def _conv_a_kernel(bg_ref, cg_ref, u_ref, cgp_ref, up_ref, cgn_ref, un_ref, w_ref, o_ref,
                   *, n_lat, n_tot, tm):
    row0 = pl.program_id(0) * tm
    last = BF16_SUBLANE - 1
    v = cg_ref[...].astype(F32) * u_ref[...].astype(F32)
    vp = cgp_ref[last:last + 1, :].astype(F32) * up_ref[last:last + 1, :].astype(F32)
    vn = cgn_ref[0:1, :].astype(F32) * un_ref[0:1, :].astype(F32)
    up, dn = _shifted(v, vp, vn, row0, n_lat, n_tot)
    w = w_ref[...]
    conv = w[0:1, :] * up + w[1:2, :] * v + w[2:3, :] * dn
    o_ref[...] = (bg_ref[...].astype(F32) * conv).astype(o_ref.dtype)


def _conv_xbc_kernel(x_ref, xp_ref, xn_ref, w_ref, b_ref, o_ref, *, n_lat, n_tot, tm):
    row0 = pl.program_id(0) * tm
    last = BF16_SUBLANE - 1
    v = x_ref[...].astype(F32)
    up, dn = _shifted(v, xp_ref[last:last + 1, :].astype(F32), xn_ref[0:1, :].astype(F32),
                      row0, n_lat, n_tot)
    w = w_ref[...]
    conv = w[0:1, :] * up + w[1:2, :] * v + w[2:3, :] * dn + b_ref[...]
    o_ref[...] = _silu(conv).astype(o_ref.dtype)


def _halo_specs(tm, cw, t, col_of):
    hb = BF16_SUBLANE
    nb = t // hb
    prev = pl.BlockSpec((hb, cw), lambda i, j: (jnp.maximum(i * (tm // hb) - 1, 0), col_of(j)))
    nxt = pl.BlockSpec((hb, cw), lambda i, j: (jnp.minimum((i + 1) * (tm // hb), nb - 1), col_of(j)))
    return prev, nxt


def _conv_a(p, conv_w, n_lat):
    t = p.shape[0]
    tm = _pick(t, (768, 512, 256))
    cw = CONV_W
    kern = functools.partial(_conv_a_kernel, n_lat=n_lat, n_tot=t, tm=tm)
    cgp, cgn = _halo_specs(tm, cw, t, lambda j: P_CG // cw)
    upv, unx = _halo_specs(tm, cw, t, lambda j: P_U // cw)
    return pl.pallas_call(
        kern,
        grid=(t // tm, 1),
        in_specs=[
            pl.BlockSpec((tm, cw), lambda i, j: (i, P_BG // cw)),
            pl.BlockSpec((tm, cw), lambda i, j: (i, P_CG // cw)),
            pl.BlockSpec((tm, cw), lambda i, j: (i, P_U // cw)),
            cgp, upv, cgn, unx,
            pl.BlockSpec((3, cw), lambda i, j: (0, 0)),
        ],
        out_specs=pl.BlockSpec((tm, cw), lambda i, j: (i, 0)),
        out_shape=jax.ShapeDtypeStruct((t, cw), BF16),
        compiler_params=_cparams(("parallel", "parallel")),
        name="conv_a",
    )(p, p, p, p, p, p, p, conv_w)


def _conv_xbc(p, conv_w, conv_b, n_lat):
    t = p.shape[0]
    tm = _pick(t, (768, 512, 256))
    cw = 512
    kern = functools.partial(_conv_xbc_kernel, n_lat=n_lat, n_tot=t, tm=tm)
    xp, xn = _halo_specs(tm, cw, t, lambda j: P_XBC // cw + j)
    return pl.pallas_call(
        kern,
        grid=(t // tm, SSD_XBC_W // cw),
        in_specs=[
            pl.BlockSpec((tm, cw), lambda i, j: (i, P_XBC // cw + j)),
            xp, xn,
            pl.BlockSpec((3, cw), lambda i, j: (0, j)),
            pl.BlockSpec((1, cw), lambda i, j: (0, j)),
        ],
        out_specs=pl.BlockSpec((tm, cw), lambda i, j: (i, j)),
        out_shape=jax.ShapeDtypeStruct((t, SSD_XBC_W), BF16),
        compiler_params=_cparams(("parallel", "parallel")),
        name="conv_xbc",
    )(p, p, p, conv_w, conv_b.reshape(1, -1))


def _rope(x, cos, sin_signed):
    lane = lax.broadcasted_iota(jnp.int32, x.shape, 1)
    first = (lane % 32) < 16
    rot = jnp.where(first, pltpu.roll(x, LANE - 16, 1), pltpu.roll(x, 16, 1))
    return x * cos + rot * sin_signed


def _rope_diff_kernel(q_ref, k_ref, cos_ref, sin_ref, qo_ref, ko_ref):
    cos = cos_ref[...]
    sin = sin_ref[...]
    q = _rope(q_ref[...].astype(F32), cos, sin) * (DIFF_SCALE * LOG2E)
    lane = lax.broadcasted_iota(jnp.int32, q.shape, 1)
    lo = lane < DIFF_QK_DIM
    qo_ref[0] = jnp.where(lo, q, 0.0).astype(BF16)
    qo_ref[1] = jnp.where(lo, 0.0, q).astype(BF16)
    ko_ref[...] = _rope(k_ref[...].astype(F32), cos, sin).astype(BF16)


def _rope_diff(p, cos_t, sin_t):
    t = p.shape[0]
    tm = _pick(t, (768, 512, 256))
    h = DIFF_HEADS
    return pl.pallas_call(
        _rope_diff_kernel,
        grid=(t // tm, h),
        in_specs=[
            pl.BlockSpec((tm, LANE), lambda i, hh: (i, P_Q // LANE + hh)),
            pl.BlockSpec((tm, LANE), lambda i, hh: (i, P_K // LANE + hh)),
            pl.BlockSpec((tm, LANE), lambda i, hh: (i, 0)),
            pl.BlockSpec((tm, LANE), lambda i, hh: (i, 0)),
        ],
        out_specs=[
            pl.BlockSpec((None, 2, tm, LANE), lambda i, hh: (hh, 0, i, 0)),
            pl.BlockSpec((None, tm, LANE), lambda i, hh: (hh, i, 0)),
        ],
        out_shape=[
            jax.ShapeDtypeStruct((h, 2, t, LANE), BF16),
            jax.ShapeDtypeStruct((h, t, LANE), BF16),
        ],
        compiler_params=_cparams(("parallel", "parallel")),
        name="rope_diff",
    )(p, p, cos_t, sin_t)


def _flash_kernel(*refs, nm, diff, lam_init):
    if diff:
        q_ref, k_ref, vt_ref, lam_ref, sub_ref, o_ref, m_scr, l_scr, acc_scr = refs
    else:
        q_ref, k_ref, vt_ref, o_ref, m_scr, l_scr, acc_scr = refs
    j = pl.program_id(2)

    @pl.when(j == 0)
    def _():
        m_scr[...] = jnp.full(m_scr.shape, -jnp.inf, F32)
        l_scr[...] = jnp.zeros(l_scr.shape, F32)
        acc_scr[...] = jnp.zeros(acc_scr.shape, F32)

    k = k_ref[...]
    vt = vt_ref[...]
    for m in range(nm):
        s = lax.dot_general(k, q_ref[m], (((1,), (1,)), ((), ())), preferred_element_type=F32)
        m_prev = m_scr[m]
        m_new = jnp.maximum(m_prev, jnp.max(s, axis=0, keepdims=True))
        alpha = jnp.exp2(m_prev - m_new)
        pmat = jnp.exp2(s - m_new)
        l_scr[m] = alpha * l_scr[m] + jnp.sum(pmat, axis=0, keepdims=True)
        acc_scr[m] = alpha * acc_scr[m] + jnp.dot(vt, pmat.astype(BF16), preferred_element_type=F32)
        m_scr[m] = m_new

    @pl.when(j == pl.num_programs(2) - 1)
    def _():
        if diff:
            lp = lam_ref[...]
            lam = (jnp.exp(jnp.sum(lp[0:1] * lp[1:2], keepdims=True))
                   - jnp.exp(jnp.sum(lp[2:3] * lp[3:4], keepdims=True)) + lam_init)
            o = acc_scr[0] / l_scr[0] - lam * (acc_scr[1] / l_scr[1])
            ms = jnp.mean(o * o, axis=0, keepdims=True)
            o = o * lax.rsqrt(ms + RMS_EPS) * sub_ref[...] * (1.0 - lam_init)
        else:
            o = acc_scr[0] / l_scr[0]
        o_ref[...] = o.T.astype(o_ref.dtype)


def _flash(q, k, vt, *, tq, tk, q_blk0, nq, k_blk0, nk, n_rows, diff, lam_init=0.0,
           lam_params=None, subln=None, prev_out=None):
    h, nm, _, dk = q.shape
    dv = vt.shape[1]
    kern = functools.partial(_flash_kernel, nm=nm, diff=diff, lam_init=lam_init)
    in_specs = [
        pl.BlockSpec((None, nm, tq, dk), lambda hh, i, j: (hh, 0, q_blk0 + i, 0)),
        pl.BlockSpec((None, tk, dk), lambda hh, i, j: (hh, k_blk0 + j, 0)),
        pl.BlockSpec((None, dv, tk), lambda hh, i, j: (hh, 0, k_blk0 + j)),
    ]
    args = [q, k, vt]
    if diff:
        in_specs += [pl.BlockSpec(lam_params.shape, lambda hh, i, j: (0, 0)),
                     pl.BlockSpec(subln.shape, lambda hh, i, j: (0, 0))]
        args += [lam_params, subln]
    aliases = {}
    if prev_out is not None:
        in_specs.append(pl.BlockSpec(memory_space=pl.ANY))
        aliases = {len(args): 0}
        args.append(prev_out)
        inner = kern

        def kern(*refs):
            n_in = len(args)
            inner(*refs[:n_in - 1], *refs[n_in:])

    return pl.pallas_call(
        kern,
        grid=(h, nq, nk),
        in_specs=in_specs,
        out_specs=pl.BlockSpec((tq, dv), lambda hh, i, j: (q_blk0 + i, hh)),
        out_shape=jax.ShapeDtypeStruct((n_rows, h * dv), BF16),
        scratch_shapes=[pltpu.VMEM((nm, 1, tq), F32), pltpu.VMEM((nm, 1, tq), F32),
                        pltpu.VMEM((nm, dv, tq), F32)],
        input_output_aliases=aliases,
        compiler_params=_cparams(("parallel", "parallel", "arbitrary")),
        name="flash_diff" if diff else "flash_mla",
    )(*args)


def _attention(q, k, vt, n_lat, n_ctx, need_ctx, **kw):
    t = n_lat + n_ctx
    tq = _pick(n_lat, (1024, 512, 256))
    tk = _pick(t, (768, 512, 256))
    out = _flash(q, k, vt, tq=tq, tk=tk, q_blk0=0, nq=n_lat // tq, k_blk0=0, nk=t // tk,
                 n_rows=t, **kw)
    if need_ctx:
        out = _flash(q, k, vt, tq=n_ctx, tk=n_ctx, q_blk0=n_lat // n_ctx, nq=1,
                     k_blk0=n_lat // n_ctx, nk=1, n_rows=t, prev_out=out, **kw)
    return out


def _ssd_kernel(xbc_ref, dtr_ref, dtt_ref, brow_ref, bcol_ref, alrow_ref, alcol_ref, alexp_ref,
                e_ref, y_ref, st_scr, *, rev):
    @pl.when(pl.program_id(0) == 0)
    def _():
        st_scr[...] = jnp.zeros(st_scr.shape, F32)

    L = SSD_CHUNK
    dt = _softplus(dtr_ref[...] + brow_ref[...])
    a = dt * (-jnp.exp(alrow_ref[...]))
    a = jnp.where(lax.broadcasted_iota(jnp.int32, a.shape, 1) < SSD_HEADS, a, 0.0)
    r = lax.broadcasted_iota(jnp.int32, (L, L), 0)
    c = lax.broadcasted_iota(jnp.int32, (L, L), 1)
    tri = (c <= r).astype(F32)
    acum = jnp.dot(tri, a, precision=HIGHEST, preferred_element_type=F32)
    e = e_ref[...]
    a_exp = jnp.dot(acum, e, precision=HIGHEST, preferred_element_type=F32)
    dt_exp = jnp.dot(dt, e, precision=HIGHEST, preferred_element_type=F32)
    dtt = _softplus(dtt_ref[...] + bcol_ref[...])
    at = dtt * (-jnp.exp(alcol_ref[...]))
    acum_t = jnp.dot(at, (r <= c).astype(F32), precision=HIGHEST, preferred_element_type=F32)
    tot = a_exp[L - 1:L, :]
    if rev:
        e_exp = a_exp - dt_exp * (-jnp.exp(alexp_ref[...]))
        col = acum - a
        row = acum_t - at
        off_f = jnp.exp(tot - e_exp)
        st_f = jnp.exp(e_exp)
    else:
        col = acum
        row = acum_t
        off_f = jnp.exp(a_exp)
        st_f = jnp.exp(tot - a_exp)
    etot = jnp.exp(tot)
    xbc = xbc_ref[...]
    xdt = xbc[:, 0:SSD_INNER].astype(F32) * dt_exp
    xs = (xdt * st_f).astype(BF16)
    lo = lax.broadcasted_iota(jnp.int32, (L, LANE), 1) < SSD_HEAD_DIM
    nt = (((1,), (1,)), ((), ()))
    for g in range(SSD_GROUPS):
        bg = xbc[:, SSD_INNER + g * SSD_STATE:SSD_INNER + (g + 1) * SSD_STATE]
        c0 = SSD_INNER + SSD_GROUPS * SSD_STATE
        cg = xbc[:, c0 + g * SSD_STATE:c0 + (g + 1) * SSD_STATE]
        gmat = lax.dot_general(cg, bg, nt, preferred_element_type=F32)
        bt = bg.astype(F32).T.astype(BF16)
        for pr in range(2):
            pair = g * 2 + pr
            sl = slice(pair * LANE, (pair + 1) * LANE)
            xp = xdt[:, sl]
            yd = jnp.zeros((L, LANE), F32)
            for hh in range(2):
                hd = pair * 2 + hh
                if rev:
                    dmat = row[hd:hd + 1, :] - col[:, hd:hd + 1]
                    mask = c >= r
                else:
                    dmat = col[:, hd:hd + 1] - row[hd:hd + 1, :]
                    mask = c <= r
                lm = jnp.where(mask, jnp.exp(dmat), 0.0)
                mm = (gmat * lm).astype(BF16)
                xm = jnp.where(lo if hh == 0 else jnp.logical_not(lo), xp, 0.0).astype(BF16)
                yd = yd + jnp.dot(mm, xm, preferred_element_type=F32)
            st = st_scr[pair]
            yo = jnp.dot(cg, st.astype(BF16), preferred_element_type=F32) * off_f[:, sl]
            y_ref[:, sl] = yd + yo
            st_scr[pair] = st * etot[:, sl] + jnp.dot(bt, xs[:, sl], preferred_element_type=F32)


def _ssd(xbc_act, dtr, dtt, dt_bias, a_log, expand, n_lat, direction):
    t = xbc_act.shape[0]
    L = SSD_CHUNK
    nc = t // L
    nl = n_lat // L
    rev = direction == 1
    if rev:
        chunk = lambda k: nc - 1 - k
    else:
        chunk = lambda k: (k + nl) % nc
    pad = LANE - SSD_HEADS
    brow = jnp.pad(dt_bias[direction], (0, pad)).reshape(1, LANE)
    bcol = dt_bias[direction].reshape(SSD_HEADS, 1)
    alrow = jnp.pad(a_log[direction], (0, pad)).reshape(1, LANE)
    alcol = a_log[direction].reshape(SSD_HEADS, 1)
    alexp = jnp.repeat(a_log[direction], SSD_HEAD_DIM).reshape(1, SSD_INNER)
    const = lambda shape: pl.BlockSpec(shape, lambda k: (0,) * len(shape))
    return pl.pallas_call(
        functools.partial(_ssd_kernel, rev=rev),
        grid=(nc,),
        in_specs=[
            pl.BlockSpec((L, SSD_XBC_W), lambda k: (chunk(k), 0)),
            pl.BlockSpec((L, LANE), lambda k: (chunk(k), direction)),
            pl.BlockSpec((SSD_HEADS, L), lambda k: (direction, chunk(k))),
            const((1, LANE)), const((SSD_HEADS, 1)), const((1, LANE)), const((SSD_HEADS, 1)),
            const((1, SSD_INNER)), const((LANE, SSD_INNER)),
        ],
        out_specs=pl.BlockSpec((L, SSD_INNER), lambda k: (chunk(k), 0)),
        out_shape=jax.ShapeDtypeStruct((t, SSD_INNER), F32),
        scratch_shapes=[pltpu.VMEM((SSD_HEADS // 2, SSD_STATE, LANE), F32)],
        compiler_params=_cparams(("arbitrary",)),
        name="ssd_rev" if rev else "ssd_fwd",
    )(xbc_act, dtr, dtt, brow, bcol, alrow, alcol, alexp, expand)


def _gate_norm_kernel(yf_ref, yb_ref, x_ref, z_ref, d_ref, nw_ref, o_ref):
    y = yf_ref[...] + yb_ref[...] + x_ref[...].astype(F32) * d_ref[...]
    g = y * _silu(z_ref[...].astype(F32))
    gw = SSD_INNER // SSD_GROUPS
    outs = []
    for i in range(SSD_GROUPS):
        gi = g[:, i * gw:(i + 1) * gw]
        outs.append(gi * lax.rsqrt(jnp.mean(gi * gi, axis=-1, keepdims=True) + RMS_EPS))
    o_ref[...] = (jnp.concatenate(outs, axis=-1) * nw_ref[...]).astype(o_ref.dtype)


def _gate_norm(yf, yb, xbc_act, p, d_exp, norm_w, n_rows):
    tm = _pick(n_rows, (1024, 768, 512, 256))
    w = SSD_INNER
    return pl.pallas_call(
        _gate_norm_kernel,
        grid=(n_rows // tm,),
        in_specs=[
            pl.BlockSpec((tm, w), lambda i: (i, 0)),
            pl.BlockSpec((tm, w), lambda i: (i, 0)),
            pl.BlockSpec((tm, w), lambda i: (i, 0)),
            pl.BlockSpec((tm, w), lambda i: (i, P_Z // w)),
            pl.BlockSpec((1, w), lambda i: (0, 0)),
            pl.BlockSpec((1, w), lambda i: (0, 0)),
        ],
        out_specs=pl.BlockSpec((tm, w), lambda i: (i, 0)),
        out_shape=jax.ShapeDtypeStruct((xbc_act.shape[0], w), BF16),
        compiler_params=_cparams(("parallel",)),
        name="gate_norm",
    )(yf, yb, xbc_act, p, d_exp, norm_w.reshape(1, w))


def _mla_up_kernel(cq_ref, ckv_ref, kr_ref, cos_ref, sin_ref, qnw_ref, kvnw_ref, wq_ref, wkn_ref,
                   wvt_ref, q_ref, k_ref, vt_ref):
    cos = cos_ref[...]
    sin = sin_ref[...]
    cq = cq_ref[...][:, 0:MLA_Q_RANK].astype(F32)
    cqn = cq * lax.rsqrt(jnp.mean(cq * cq, axis=-1, keepdims=True) + RMS_EPS) * qnw_ref[...]
    q = jnp.dot(cqn.astype(BF16), wq_ref[...], preferred_element_type=F32)
    ckv = ckv_ref[...].astype(F32)
    ckvn = (ckv * lax.rsqrt(jnp.mean(ckv * ckv, axis=-1, keepdims=True) + RMS_EPS)
            * kvnw_ref[...]).astype(BF16)
    kn = jnp.dot(ckvn, wkn_ref[...], preferred_element_type=F32)
    kr = _rope(kr_ref[...].astype(F32), cos, sin).astype(BF16)
    hw = 2 * LANE
    for hd in range(MLA_HEADS):
        qn = q[:, hd * hw:hd * hw + LANE]
        qr = _rope(q[:, hd * hw + LANE:(hd + 1) * hw], cos, sin)
        q_ref[hd, 0] = (jnp.concatenate([qn, qr], axis=-1) * (MLA_SCALE * LOG2E)).astype(BF16)
        k_ref[hd] = jnp.concatenate([kn[:, hd * LANE:(hd + 1) * LANE].astype(BF16), kr], axis=-1)
    vt = lax.dot_general(wvt_ref[...], ckvn, (((1,), (1,)), ((), ())), preferred_element_type=F32)
    vt_ref[...] = vt.astype(BF16)


def _mla_up(p, cos_t, sin_t, q_norm_w, kv_norm_w, wq_pad, w_kn, w_vt):
    t = p.shape[0]
    tm = _pick(t, (768, 512, 256))
    h = MLA_HEADS
    hw = 2 * LANE
    const = lambda a: pl.BlockSpec(a.shape, lambda i: (0,) * a.ndim)
    qnw = q_norm_w.reshape(1, -1)
    kvnw = kv_norm_w.reshape(1, -1)
    return pl.pallas_call(
        _mla_up_kernel,
        grid=(t // tm,),
        in_specs=[
            pl.BlockSpec((tm, 512), lambda i: (i, P_CQ // 512)),
            pl.BlockSpec((tm, MLA_KV_RANK), lambda i: (i, P_CKV // MLA_KV_RANK)),
            pl.BlockSpec((tm, LANE), lambda i: (i, P_KR // LANE)),
            pl.BlockSpec((tm, LANE), lambda i: (i, 0)),
            pl.BlockSpec((tm, LANE), lambda i: (i, 0)),
            const(qnw), const(kvnw), const(wq_pad), const(w_kn), const(w_vt),
        ],
        out_specs=[
            pl.BlockSpec((h, 1, tm, hw), lambda i: (0, 0, i, 0)),
            pl.BlockSpec((h, tm, hw), lambda i: (0, i, 0)),
            pl.BlockSpec((h * MLA_V, tm), lambda i: (0, i)),
        ],
        out_shape=[
            jax.ShapeDtypeStruct((h, 1, t, hw), BF16),
            jax.ShapeDtypeStruct((h, t, hw), BF16),
            jax.ShapeDtypeStruct((h * MLA_V, t), BF16),
        ],
        compiler_params=_cparams(("parallel",)),
        name="mla_up",
    )(p, p, p, cos_t, sin_t, qnw, kvnw, wq_pad, w_kn, w_vt)


def _wout_kernel(a_ref, b_ref, c_ref, d_ref, x_ref, mod_ref, w_ref, g_ref, bb_ref, x1_ref, h2_ref,
                 *, n_lat, tm, d, alpha):
    is_ctx = _row_is_ctx(pl.program_id(0) * tm, tm, n_lat)
    mod = mod_ref[...]
    gw = GROUP_W
    y = jnp.dot(a_ref[...], w_ref[0:gw, :], preferred_element_type=F32)
    y += jnp.dot(b_ref[...], w_ref[gw:2 * gw, :], preferred_element_type=F32)
    y += jnp.dot(c_ref[...], w_ref[2 * gw:3 * gw, :], preferred_element_type=F32)
    y += jnp.dot(d_ref[...], w_ref[3 * gw:4 * gw, :], preferred_element_type=F32)
    r = alpha * x_ref[...] + _mod_rows(mod, is_ctx, 2, d) * y
    x1 = _layer_norm(r, g_ref[...], bb_ref[...])
    x1_ref[...] = x1
    h2_ref[...] = (x1 * (1.0 + _mod_rows(mod, is_ctx, 4, d)) + _mod_rows(mod, is_ctx, 3, d)).astype(BF16)


def _wout(mixes, xall, mods, layer, w_out, ln_g, ln_b, n_lat, n_rows, alpha):
    d = xall.shape[1]
    tm = _pick(n_rows, (512, 384, 256))
    n6 = mods.shape[-1]
    kern = functools.partial(_wout_kernel, n_lat=n_lat, tm=tm, d=d, alpha=alpha)
    mix_spec = pl.BlockSpec((tm, GROUP_W), lambda i: (i, 0))
    return pl.pallas_call(
        kern,
        grid=(n_rows // tm,),
        in_specs=[
            mix_spec, mix_spec, mix_spec, mix_spec,
            pl.BlockSpec((tm, d), lambda i: (i, 0)),
            pl.BlockSpec((None, 8, n6), lambda i: (layer, 0, 0)),
            pl.BlockSpec(w_out.shape, lambda i: (0, 0)),
            pl.BlockSpec((1, d), lambda i: (0, 0)),
            pl.BlockSpec((1, d), lambda i: (0, 0)),
        ],
        out_specs=[
            pl.BlockSpec((tm, d), lambda i: (i, 0)),
            pl.BlockSpec((tm, d), lambda i: (i, 0)),
        ],
        out_shape=[
            jax.ShapeDtypeStruct((n_rows, d), F32),
            jax.ShapeDtypeStruct((n_rows, d), BF16),
        ],
        compiler_params=_cparams(("parallel",)),
        name="wout_ln1",
    )(*mixes, xall, mods, w_out, ln_g.reshape(1, d), ln_b.reshape(1, d))


def _ffn1_kernel(h_ref, wg_ref, wu_ref, o_ref):
    h = h_ref[...]
    g = jnp.dot(h, wg_ref[...], preferred_element_type=F32)
    u = jnp.dot(h, wu_ref[...], preferred_element_type=F32)
    o_ref[...] = (_silu(g) * u).astype(o_ref.dtype)


def _ffn1(h2, w1):
    n_rows, d = h2.shape
    dff = w1.shape[1] // 2
    tm = _pick(n_rows, (1024, 768, 512, 256))
    tf = _pick(dff, (512, 256, 128))
    nf = dff // tf
    return pl.pallas_call(
        _ffn1_kernel,
        grid=(n_rows // tm, nf),
        in_specs=[
            pl.BlockSpec((tm, d), lambda i, j: (i, 0)),
            pl.BlockSpec((d, tf), lambda i, j: (0, j)),
            pl.BlockSpec((d, tf), lambda i, j: (0, nf + j)),
        ],
        out_specs=pl.BlockSpec((tm, tf), lambda i, j: (i, j)),
        out_shape=jax.ShapeDtypeStruct((n_rows, dff), BF16),
        compiler_params=_cparams(("parallel", "parallel")),
        name="ffn1",
    )(h2, w1, w1)


def _ffn2_kernel(a_ref, w_ref, x_ref, mod_ref, g_ref, bb_ref, o_ref, acc_scr, *, n_lat, tm, d, alpha):
    k = pl.program_id(1)

    @pl.when(k == 0)
    def _():
        acc_scr[...] = jnp.zeros(acc_scr.shape, F32)

    acc_scr[...] += jnp.dot(a_ref[...], w_ref[...], preferred_element_type=F32)

    @pl.when(k == pl.num_programs(1) - 1)
    def _():
        is_ctx = _row_is_ctx(pl.program_id(0) * tm, tm, n_lat)
        r = alpha * x_ref[...] + _mod_rows(mod_ref[...], is_ctx, 5, d) * acc_scr[...]
        o_ref[...] = _layer_norm(r, g_ref[...], bb_ref[...])


def _ffn2(act, w2, x1, mods, layer, ln_g, ln_b, n_lat, alpha):
    n_rows, dff = act.shape
    d = w2.shape[1]
    tm = _pick(n_rows, (768, 512, 256))
    tk = _pick(dff, (512, 256, 128))
    n6 = mods.shape[-1]
    kern = functools.partial(_ffn2_kernel, n_lat=n_lat, tm=tm, d=d, alpha=alpha)
    return pl.pallas_call(
        kern,
        grid=(n_rows // tm, dff // tk),
        in_specs=[
            pl.BlockSpec((tm, tk), lambda i, k: (i, k)),
            pl.BlockSpec((tk, d), lambda i, k: (k, 0)),
            pl.BlockSpec((tm, d), lambda i, k: (i, 0)),
            pl.BlockSpec((None, 8, n6), lambda i, k: (layer, 0, 0)),
            pl.BlockSpec((1, d), lambda i, k: (0, 0)),
            pl.BlockSpec((1, d), lambda i, k: (0, 0)),
        ],
        out_specs=pl.BlockSpec((tm, d), lambda i, k: (i, 0)),
        out_shape=jax.ShapeDtypeStruct((n_rows, d), F32),
        scratch_shapes=[pltpu.VMEM((tm, d), F32)],
        compiler_params=_cparams(("parallel", "arbitrary")),
        name="ffn2_ln2",
    )(act, w2, x1, mods, ln_g.reshape(1, d), ln_b.reshape(1, d))


def _rope_tables(n_lat, n_ctx):
    n_rows = n_lat // GRID_W
    row = jnp.repeat(jnp.arange(n_rows, dtype=F32), GRID_W)
    col = jnp.tile(jnp.arange(GRID_W, dtype=F32), n_rows)
    axis_dim = ROPE_DIM // 2
    inv_freq = jnp.power(ROPE_BASE, -jnp.arange(0, axis_dim, 2, dtype=F32) / axis_dim)
    ang_r = row[:, None] * inv_freq
    ang_c = col[:, None] * inv_freq
    ang = jnp.concatenate([ang_r, ang_r, ang_c, ang_c], axis=-1)
    cos = jnp.tile(jnp.cos(ang), (1, LANE // ROPE_DIM))
    sin = jnp.tile(jnp.sin(ang), (1, LANE // ROPE_DIM))
    sign = jnp.where((jnp.arange(LANE) % 32) < 16, -1.0, 1.0).astype(F32)
    cos_t = jnp.concatenate([cos, jnp.ones((n_ctx, LANE), F32)], axis=0)
    sin_t = jnp.concatenate([sin * sign, jnp.zeros((n_ctx, LANE), F32)], axis=0)
    return cos_t, sin_t


def _split_w_in(w_in):
    d = w_in.shape[0]
    sizes = (512, 512, 512, 512, 512, 512, 512, 1024, 16, 384, 256, 64)
    offs = [0]
    for s in sizes:
        offs.append(offs[-1] + s)
    seg = [w_in[:, offs[i]:offs[i + 1]] for i in range(len(sizes))]
    z = lambda n: jnp.zeros((d, n), w_in.dtype)
    w_main = jnp.concatenate(
        [seg[7], seg[0], seg[1], seg[2], seg[3], seg[4], seg[6], seg[9], z(512 - 384), seg[10],
         seg[11], z(LANE - 64), z(P_W - P_KR - LANE)], axis=1).astype(BF16)
    w_vt = seg[5].T.astype(BF16)
    w_dtr = jnp.concatenate([seg[8][:, 0:8], z(LANE - 8), seg[8][:, 8:16], z(LANE - 8)], axis=1).astype(BF16)
    w_dtt = seg[8].T.astype(BF16)
    return w_main, w_vt, w_dtr, w_dtt


def _split_mla(w_q_up, w_kv_up):
    h = MLA_HEADS
    rq = w_q_up.shape[0]
    wq = w_q_up.reshape(rq, h, MLA_NOPE + MLA_ROPE)
    wq_pad = jnp.concatenate([wq, jnp.zeros((rq, h, 2 * LANE - MLA_NOPE - MLA_ROPE), wq.dtype)], axis=-1)
    wq_pad = wq_pad.reshape(rq, h * 2 * LANE).astype(BF16)
    rk = w_kv_up.shape[0]
    wkv = w_kv_up.reshape(rk, h, MLA_NOPE + MLA_V)
    w_kn = wkv[:, :, :MLA_NOPE].reshape(rk, h * MLA_NOPE).astype(BF16)
    w_vt = wkv[:, :, MLA_NOPE:].reshape(rk, h * MLA_V).T.astype(BF16)
    return wq_pad, w_kn, w_vt


def kernel(x, c, ctx, c_ctx, w_ada, b_ada, w_in, conv_a_w, diff_lambda, diff_subln_w, ssd_conv_w,
           ssd_conv_b, ssd_dt_bias, ssd_a_log, ssd_d, ssd_norm_w, mla_q_norm_w, mla_kv_norm_w,
           w_q_up, w_kv_up, w_out, ln1_g, ln1_b, w_ffn_in, w_ffn_out, ln2_g, ln2_b):
    bsz, n_lat, d = x.shape
    n_ctx = ctx.shape[1]
    depth = w_ada.shape[0]
    if bsz != 1:
        raise ValueError("kernel handles batch 1")
    t = n_lat + n_ctx
    alpha = (2 * depth) ** 0.25

    cvec = jnp.concatenate([c, c_ctx[None, :], jnp.zeros((6, d), F32)], axis=0)
    mods = _ada(cvec, w_ada, b_ada)
    cos_t, sin_t = _rope_tables(n_lat, n_ctx)
    expand = (jnp.arange(LANE)[:, None] == (jnp.arange(SSD_INNER)[None, :] // SSD_HEAD_DIM)).astype(F32)
    xall = jnp.concatenate([x[0], ctx[0]], axis=0)

    for layer in range(depth):
        need_ctx = layer < depth - 1
        n_rows = t if need_ctx else n_lat
        w_main, w_vt, w_dtr, w_dtt = _split_w_in(w_in[layer])
        p, vt_diff, dtr, dtt = _inproj(xall, mods, layer, w_main, w_vt, w_dtr, w_dtt, n_lat)

        a_mix = _conv_a(p, conv_a_w[layer], n_lat)

        q_d, k_d = _rope_diff(p, cos_t, sin_t)
        lam_init = 0.8 - 0.6 * math.exp(-0.3 * layer)
        b_mix = _attention(q_d, k_d, vt_diff.reshape(DIFF_HEADS, DIFF_V_DIM, t), n_lat, n_ctx, need_ctx,
                           diff=True, lam_init=lam_init, lam_params=diff_lambda[layer],
                           subln=diff_subln_w[layer].reshape(DIFF_V_DIM, 1))

        xbc_act = _conv_xbc(p, ssd_conv_w[layer], ssd_conv_b[layer], n_lat)
        y_f = _ssd(xbc_act, dtr, dtt, ssd_dt_bias[layer], ssd_a_log[layer], expand, n_lat, 0)
        y_b = _ssd(xbc_act, dtr, dtt, ssd_dt_bias[layer], ssd_a_log[layer], expand, n_lat, 1)
        d_exp = jnp.repeat(ssd_d[layer], SSD_HEAD_DIM).reshape(1, SSD_INNER)
        c_mix = _gate_norm(y_f, y_b, xbc_act, p, d_exp, ssd_norm_w[layer], n_rows)

        wq_pad, w_kn, w_mvt = _split_mla(w_q_up[layer], w_kv_up[layer])
        q_m, k_m, vt_m = _mla_up(p, cos_t, sin_t, mla_q_norm_w[layer], mla_kv_norm_w[layer],
                                 wq_pad, w_kn, w_mvt)
        d_mix = _attention(q_m, k_m, vt_m.reshape(MLA_HEADS, MLA_V, t), n_lat, n_ctx, need_ctx, diff=False)

        x1, h2 = _wout((a_mix, b_mix, c_mix, d_mix), xall, mods, layer, w_out[layer].astype(BF16),
                       ln1_g[layer], ln1_b[layer], n_lat, n_rows, alpha)
        act = _ffn1(h2, w_ffn_in[layer].astype(BF16))
        xall = _ffn2(act, w_ffn_out[layer].astype(BF16), x1, mods, layer, ln2_g[layer], ln2_b[layer],
                     n_lat, alpha)
    return xall[:n_lat][None]
```
